```python
import math
import jax, jax.numpy as jnp
from jax import lax
import numpy as np

D_MODEL = 1024
BATCH = 4
SEQ = 8192
DEPTH = 1

S5_WIDTH = D_MODEL // 2
S5_GROUP = 16
S5_GROUPS = S5_WIDTH // S5_GROUP
S5_STATE = 64
DT_MIN = 0.001
DT_MAX = 0.1

N_Q_HEADS = 8
N_KV_GROUPS = 2
HEADS_PER_GROUP = N_Q_HEADS // N_KV_GROUPS
HEAD_DIM = 64
NSA_WIDTH = N_Q_HEADS * HEAD_DIM
KV_WIDTH = N_KV_GROUPS * HEAD_DIM
CMP_BLOCK = 32
CMP_STRIDE = 16
CMP_HIDDEN = 2 * HEAD_DIM
SEL_BLOCK = 64
SEL_TOPK = 16
WINDOW = 512
Q_BLK = 128
ATTN_SCALE = HEAD_DIM ** -0.5
SEL_FORCE = 1e9

N_BUCKETS = 32
REL_MAX_DIST = 1024

N_EXPERTS = 32
TOP_K = 4
D_FF = D_MODEL
SWIGLU_LIMIT = 7.0
SWIGLU_ALPHA = 1.702
MOE_BLK = 128

DEEPNORM_ALPHA = (2 * DEPTH) ** 0.25
DEEPNORM_BETA = (8 * DEPTH) ** -0.25
LN_EPS = 1e-5

IN_WIDTHS = (S5_WIDTH, NSA_WIDTH, KV_WIDTH, KV_WIDTH, KV_WIDTH, KV_WIDTH, KV_WIDTH, KV_WIDTH, 3 * N_Q_HEADS, D_MODEL, D_MODEL)
IN_SPLITS = tuple(int(v) for v in np.cumsum(IN_WIDTHS)[:-1])
D_IN = int(sum(IN_WIDTHS))

kernel_name = "hybrid_s5_nsa_moe_block"


def layer_norm(x, g, b):
    xf = x.astype(jnp.float32)
    mu = xf.mean(-1, keepdims=True)
    var = jnp.square(xf - mu).mean(-1, keepdims=True)
    return ((xf - mu) * lax.rsqrt(var + LN_EPS) * g.astype(jnp.float32) + b.astype(jnp.float32)).astype(x.dtype)


def masked_softmax(logits, mask):
    z = jnp.where(mask, logits.astype(jnp.float32), -jnp.inf)
    m = jnp.max(z, axis=-1, keepdims=True)
    m = jnp.where(jnp.isfinite(m), m, 0.0)
    p = jnp.exp(z - m)
    return p / jnp.maximum(p.sum(-1, keepdims=True), 1e-30)


def rel_bucket(dist):
    n = jnp.maximum(dist, 0)
    max_exact = N_BUCKETS // 2
    nf = jnp.maximum(n, 1).astype(jnp.float32)
    large = max_exact + (jnp.log(nf / max_exact) / math.log(REL_MAX_DIST / max_exact) * (N_BUCKETS - max_exact)).astype(jnp.int32)
    large = jnp.minimum(large, N_BUCKETS - 1)
    return jnp.where(n < max_exact, n, large)


def s5_mixer(u, lam_re, lam_im, log_dt, b_re, b_im, c_re, c_im, d_skip, w_glu, b_glu):
    B, S, _ = u.shape
    f32 = jnp.float32
    uf = u.astype(f32).reshape(B, S, S5_GROUPS, S5_GROUP)
    lam = lax.complex(lam_re.astype(f32), lam_im.astype(f32))
    dt = jnp.exp(log_dt.astype(f32))[:, None]
    lam_bar = jnp.exp(lam * dt)
    b_mat = lax.complex(b_re.astype(f32), b_im.astype(f32))
    b_bar = ((lam_bar - 1.0) / lam)[..., None] * b_mat
    c_mat = lax.complex(c_re.astype(f32), c_im.astype(f32))
    bu = jnp.einsum('gnp,bsgp->bsgn', b_bar, uf.astype(jnp.complex64))
    a = jnp.broadcast_to(lam_bar[None, None], (1, S, S5_GROUPS, S5_STATE))

    def combine(e1, e2):
        a1, b1 = e1
        a2, b2 = e2
        return a1 * a2, a2 * b1 + b2

    _, states = lax.associative_scan(combine, (a, bu), axis=1)
    y = jnp.einsum('gpn,bsgn->bsgp', c_mat, states).real + d_skip.astype(f32).reshape(S5_GROUPS, S5_GROUP) * uf
    y = jax.nn.gelu(y.reshape(B, S, S5_WIDTH)).astype(u.dtype)
    val, gate = jnp.split(y @ w_glu + b_glu, 2, axis=-1)
    return val * jax.nn.sigmoid(gate)


def compress_blocks(k, pos, w1, w2):
    S = k.shape[2]
    n_cmp = (S - CMP_BLOCK) // CMP_STRIDE + 1
    idx = jnp.arange(n_cmp)[:, None] * CMP_STRIDE + jnp.arange(CMP_BLOCK)[None, :]
    kb = k[:, :, idx, :] + pos
    kb = kb.reshape(kb.shape[0], kb.shape[1], n_cmp, CMP_BLOCK * HEAD_DIM)
    return jax.nn.gelu(kb @ w1) @ w2


def nsa_mixer(q, k_c, v_c, k_s, v_s, k_w, v_w, g_br, rel_bias, pos_k, w1_k, w2_k, pos_v, w1_v, w2_v):
    B, G, hpg, S, dk = q.shape
    n_cmp = (S - CMP_BLOCK) // CMP_STRIDE + 1
    n_sel = S // SEL_BLOCK
    n_top = min(SEL_TOPK, n_sel)
    n_qb = S // Q_BLK

    kc = compress_blocks(k_c, pos_k, w1_k, w2_k)
    vc = compress_blocks(v_c, pos_v, w1_v, w2_v)
    cmp_start = jnp.arange(n_cmp) * CMP_STRIDE
    cmp_end = cmp_start + CMP_BLOCK - 1
    sel_start = jnp.arange(n_sel) * SEL_BLOCK
    sel_end = sel_start + SEL_BLOCK - 1
    overlap = ((cmp_start[:, None] <= sel_end[None, :]) & (cmp_end[:, None] >= sel_start[None, :])).astype(jnp.float32)

    ks_blocks = k_s.reshape(B, G, n_sel, SEL_BLOCK, dk)
    vs_blocks = v_s.reshape(B, G, n_sel, SEL_BLOCK, dk)
    kw_pad = jnp.pad(k_w, ((0, 0), (0, 0), (WINDOW, 0), (0, 0)))
    vw_pad = jnp.pad(v_w, ((0, 0), (0, 0), (WINDOW, 0), (0, 0)))
    tbl = rel_bias.astype(jnp.float32).T.reshape(G, hpg, N_BUCKETS)
    g_ix = jnp.arange(G)[None, :, None, None, None]
    h_ix = jnp.arange(hpg)[None, None, :, None, None]
    gather_sel = jax.vmap(jax.vmap(lambda blk, ix: blk[ix]))

    def qblock(i):
        q0 = i * Q_BLK
        t = q0 + jnp.arange(Q_BLK)
        qb = lax.dynamic_slice_in_dim(q, q0, Q_BLK, axis=3)
        gb = lax.dynamic_slice_in_dim(g_br, q0, Q_BLK, axis=3)

        lc = jnp.einsum('bghqd,bgnd->bghqn', qb, kc).astype(jnp.float32) * ATTN_SCALE
        lc = lc + tbl[:, :, rel_bucket(t[:, None] - cmp_end[None, :])]
        p_c = masked_softmax(lc, cmp_end[None, :] <= t[:, None])
        o_c = jnp.einsum('bghqn,bgnd->bghqd', p_c.astype(vc.dtype), vc)

        imp = jnp.einsum('bghqn,nj->bgqj', p_c, overlap)
        cur = t // SEL_BLOCK
        j = jnp.arange(n_sel)[None, :]
        forced = (j == 0) | (j == cur[:, None]) | (j == cur[:, None] - 1)
        causal_blk = sel_start[None, :] <= t[:, None]
        score = jnp.where(forced, SEL_FORCE, jnp.where(causal_blk, imp, -SEL_FORCE))
        _, sel_idx = lax.top_k(score, n_top)

        kg = gather_sel(ks_blocks, sel_idx).reshape(B, G, Q_BLK, n_top * SEL_BLOCK, dk)
        vg = gather_sel(vs_blocks, sel_idx).reshape(B, G, Q_BLK, n_top * SEL_BLOCK, dk)
        pos_s = (sel_idx[..., None] * SEL_BLOCK + jnp.arange(SEL_BLOCK)).reshape(B, G, Q_BLK, n_top * SEL_BLOCK)
        dist_s = t[None, None, :, None] - pos_s
        ls = jnp.einsum('bghqd,bgqkd->bghqk', qb, kg).astype(jnp.float32) * ATTN_SCALE
        ls = ls + tbl[g_ix, h_ix, rel_bucket(dist_s)[:, :, None]]
        p_s = masked_softmax(ls, (dist_s >= 0)[:, :, None])
        o_s = jnp.einsum('bghqk,bgqkd->bghqd', p_s.astype(vg.dtype), vg)

        kwb = lax.dynamic_slice_in_dim(kw_pad, q0, Q_BLK + WINDOW, axis=2)
        vwb = lax.dynamic_slice_in_dim(vw_pad, q0, Q_BLK + WINDOW, axis=2)
        pos_w = q0 - WINDOW + jnp.arange(Q_BLK + WINDOW)
        dist_w = t[:, None] - pos_w[None, :]
        mask_w = (dist_w >= 0) & (dist_w < WINDOW) & (pos_w[None, :] >= 0)
        lw = jnp.einsum('bghqd,bgkd->bghqk', qb, kwb).astype(jnp.float32) * ATTN_SCALE
        lw = lw + tbl[:, :, rel_bucket(dist_w)]
        p_w = masked_softmax(lw, mask_w)
        o_w = jnp.einsum('bghqk,bgkd->bghqd', p_w.astype(vwb.dtype), vwb)

        return gb[..., 0:1] * o_c + gb[..., 1:2] * o_s + gb[..., 2:3] * o_w

    out = lax.map(qblock, jnp.arange(n_qb))
    return out.transpose(1, 0, 4, 2, 3, 5).reshape(B, S, G * hpg * dk)


def token_mixer(h, rel_bias, w_in, lam_re, lam_im, log_dt, b_re, b_im, c_re, c_im, d_skip, w_glu, b_glu,
                pos_k, w1_k, w2_k, pos_v, w1_v, w2_v, w_nsa_up, w_out):
    B, S, _ = h.shape
    u, q, k_c, v_c, k_s, v_s, k_w, v_w, g_nsa, gm_s5, gm_nsa = jnp.split(h @ w_in, IN_SPLITS, axis=-1)
    y_s5 = s5_mixer(u, lam_re, lam_im, log_dt, b_re, b_im, c_re, c_im, d_skip, w_glu, b_glu)

    def heads_q(z, last):
        return z.reshape(B, S, N_KV_GROUPS, HEADS_PER_GROUP, last).transpose(0, 2, 3, 1, 4)

    def heads_kv(z):
        return z.reshape(B, S, N_KV_GROUPS, HEAD_DIM).transpose(0, 2, 1, 3)

    o_nsa = nsa_mixer(heads_q(q, HEAD_DIM), heads_kv(k_c), heads_kv(v_c), heads_kv(k_s), heads_kv(v_s),
                      heads_kv(k_w), heads_kv(v_w), jax.nn.sigmoid(heads_q(g_nsa, 3)), rel_bias,
                      pos_k, w1_k, w2_k, pos_v, w1_v, w2_v)
    y_nsa = o_nsa @ w_nsa_up
    merged = jax.nn.sigmoid(gm_s5) * y_s5 + jax.nn.sigmoid(gm_nsa) * y_nsa
    return merged @ w_out


def moe_ffn(h, w_router, b_router, w_gate_up, b_gate_up, w_down, b_down):
    B, S, D = h.shape
    T = B * S
    TK = T * TOP_K
    xt = h.reshape(T, D)
    logits = (xt @ w_router + b_router).astype(jnp.float32)
    top_val, top_idx = lax.top_k(logits, TOP_K)
    gates = jax.nn.softmax(top_val, axis=-1)
    flat_e = top_idx.reshape(-1)
    flat_tok = jnp.repeat(jnp.arange(T, dtype=jnp.int32), TOP_K)
    flat_g = gates.reshape(-1)
    order = jnp.argsort(flat_e)
    se, stok, sg = flat_e[order], flat_tok[order], flat_g[order]
    counts = jnp.zeros((N_EXPERTS,), jnp.int32).at[flat_e].add(1)
    starts = jnp.cumsum(counts) - counts
    pcounts = (counts + MOE_BLK - 1) // MOE_BLK * MOE_BLK
    pends = jnp.cumsum(pcounts)
    pstarts = pends - pcounts
    dest = pstarts[se] + jnp.arange(TK, dtype=jnp.int32) - starts[se]
    n_rows = ((TK + MOE_BLK - 1) // MOE_BLK + N_EXPERTS) * MOE_BLK
    n_blk = n_rows // MOE_BLK
    row_tok = jnp.full((n_rows,), T, jnp.int32).at[dest].set(stok)
    row_g = jnp.zeros((n_rows,), flat_g.dtype).at[dest].set(sg)
    blk_e = jnp.clip(jnp.searchsorted(pends, jnp.arange(n_blk) * MOE_BLK, side='right'), 0, N_EXPERTS - 1)
    x_pad = jnp.concatenate([xt, jnp.zeros((1, D), xt.dtype)], axis=0)
    rows = x_pad[row_tok].reshape(n_blk, MOE_BLK, D)

    def expert_block(args):
        xb, e = args
        gate, up = jnp.split(xb @ w_gate_up[e] + b_gate_up[e], 2, axis=-1)
        gate = jnp.minimum(gate, SWIGLU_LIMIT)
        up = jnp.clip(up, -SWIGLU_LIMIT, SWIGLU_LIMIT)
        act = (up + 1.0) * gate * jax.nn.sigmoid(SWIGLU_ALPHA * gate)
        return act @ w_down[e] + b_down[e]

    y_rows = lax.map(expert_block, (rows, blk_e)).reshape(n_rows, D)
    out = jax.ops.segment_sum(y_rows * row_g[:, None], row_tok, num_segments=T + 1)[:T]
    return out.astype(h.dtype).reshape(B, S, D)


def setup_inputs(seed: int = 0) -> dict:
    key = jax.random.key(seed)
    keys = iter(jax.random.split(key, 48))
    L = DEPTH

    def nrm(shape, s):
        return s * jax.random.normal(next(keys), shape, jnp.float32)

    lam_im = jnp.broadcast_to(jnp.pi * jnp.arange(S5_STATE, dtype=jnp.float32), (L, S5_GROUPS, S5_STATE))
    return {
        "x": nrm((BATCH, SEQ, D_MODEL), 1.0),
        "c": nrm((BATCH, D_MODEL), 1.0),
        "rel_bias": nrm((N_BUCKETS, N_Q_HEADS), 0.5),
        "w_ada": nrm((L, D_MODEL, 6 * D_MODEL), 0.5 * D_MODEL ** -0.5),
        "b_ada": nrm((L, 6 * D_MODEL), 0.02),
        "w_in": nrm((L, D_MODEL, D_IN), D_MODEL ** -0.5),
        "ssm_lam_re": -0.5 + nrm((L, S5_GROUPS, S5_STATE), 0.01),
        "ssm_lam_im": lam_im + nrm((L, S5_GROUPS, S5_STATE), 0.01),
        "ssm_log_dt": jax.random.uniform(next(keys), (L, S5_GROUPS), jnp.float32, math.log(DT_MIN), math.log(DT_MAX)),
        "ssm_b_re": nrm((L, S5_GROUPS, S5_STATE, S5_GROUP), (2 * S5_GROUP) ** -0.5),
        "ssm_b_im": nrm((L, S5_GROUPS, S5_STATE, S5_GROUP), (2 * S5_GROUP) ** -0.5),
        "ssm_c_re": nrm((L, S5_GROUPS, S5_GROUP, S5_STATE), (2 * S5_STATE) ** -0.5),
        "ssm_c_im": nrm((L, S5_GROUPS, S5_GROUP, S5_STATE), (2 * S5_STATE) ** -0.5),
        "ssm_d": nrm((L, S5_WIDTH), 1.0),
        "w_glu": nrm((L, S5_WIDTH, 2 * D_MODEL), S5_WIDTH ** -0.5),
        "b_glu": nrm((L, 2 * D_MODEL), 0.02),
        "cmp_pos_k": nrm((L, CMP_BLOCK, HEAD_DIM), 0.02),
        "cmp_w1_k": nrm((L, CMP_BLOCK * HEAD_DIM, CMP_HIDDEN), (CMP_BLOCK * HEAD_DIM) ** -0.5),
        "cmp_w2_k": nrm((L, CMP_HIDDEN, HEAD_DIM), CMP_HIDDEN ** -0.5),
        "cmp_pos_v": nrm((L, CMP_BLOCK, HEAD_DIM), 0.02),
        "cmp_w1_v": nrm((L, CMP_BLOCK * HEAD_DIM, CMP_HIDDEN), (CMP_BLOCK * HEAD_DIM) ** -0.5),
        "cmp_w2_v": nrm((L, CMP_HIDDEN, HEAD_DIM), CMP_HIDDEN ** -0.5),
        "w_nsa_up": nrm((L, NSA_WIDTH, D_MODEL), NSA_WIDTH ** -0.5),
        "w_out": nrm((L, D_MODEL, D_MODEL), DEEPNORM_BETA * D_MODEL ** -0.5),
        "ln1_g": 1.0 + nrm((L, D_MODEL), 0.01),
        "ln1_b": nrm((L, D_MODEL), 0.01),
        "w_router": nrm((L, D_MODEL, N_EXPERTS), D_MODEL ** -0.5),
        "b_router": nrm((L, N_EXPERTS), 0.01),
        "w_gate_up": nrm((L, N_EXPERTS, D_MODEL, 2 * D_FF), D_MODEL ** -0.5),
        "b_gate_up": nrm((L, N_EXPERTS, 2 * D_FF), 0.02),
        "w_down": nrm((L, N_EXPERTS, D_FF, D_MODEL), DEEPNORM_BETA * D_FF ** -0.5),
        "b_down": nrm((L, N_EXPERTS, D_MODEL), 0.02),
        "ln2_g": 1.0 + nrm((L, D_MODEL), 0.01),
        "ln2_b": nrm((L, D_MODEL), 0.01),
    }


def reference(x, c, rel_bias, w_ada, b_ada, w_in, ssm_lam_re, ssm_lam_im, ssm_log_dt, ssm_b_re, ssm_b_im,
              ssm_c_re, ssm_c_im, ssm_d, w_glu, b_glu, cmp_pos_k, cmp_w1_k, cmp_w2_k, cmp_pos_v, cmp_w1_v,
              cmp_w2_v, w_nsa_up, w_out, ln1_g, ln1_b, w_router, b_router, w_gate_up, b_gate_up, w_down,
              b_down, ln2_g, ln2_b):
    cond = jax.nn.silu(c)
    for l in range(DEPTH):
        ada = (cond @ w_ada[l] + b_ada[l])[:, None, :]
        sh_a, sc_a, g_a, sh_f, sc_f, g_f = jnp.split(ada, 6, axis=-1)
        h = x * (1.0 + sc_a) + sh_a
        y = token_mixer(h, rel_bias, w_in[l], ssm_lam_re[l], ssm_lam_im[l], ssm_log_dt[l], ssm_b_re[l],
                        ssm_b_im[l], ssm_c_re[l], ssm_c_im[l], ssm_d[l], w_glu[l], b_glu[l],
                        cmp_pos_k[l], cmp_w1_k[l], cmp_w2_k[l], cmp_pos_v[l], cmp_w1_v[l], cmp_w2_v[l],
                        w_nsa_up[l], w_out[l])
        x = layer_norm(DEEPNORM_ALPHA * x + g_a * y, ln1_g[l], ln1_b[l])
        h = x * (1.0 + sc_f) + sh_f
        y = moe_ffn(h, w_router[l], b_router[l], w_gate_up[l], b_gate_up[l], w_down[l], b_down[l])
        x = layer_norm(DEEPNORM_ALPHA * x + g_f * y, ln2_g[l], ln2_b[l])
    return x
```

```python
import functools
import math

import numpy as np
import jax
import jax.numpy as jnp
from jax import lax
from jax.experimental import pallas as pl
from jax.experimental.pallas import tpu as pltpu

F32 = jnp.float32
BF16 = jnp.bfloat16

D_MODEL = 1024
S5_WIDTH = 512
S5_GROUP = 16
S5_GROUPS = 32
S5_STATE = 64
S5_CHUNK = 32
N_Q_HEADS = 8
N_KV_GROUPS = 2
HPG = 4
HEAD_DIM = 64
NSA_WIDTH = 512
KV_WIDTH = 128
CMP_BLOCK = 32
CMP_STRIDE = 16
CMP_HIDDEN = 128
SEL_BLOCK = 64
SEL_TOPK = 16
WINDOW = 512
Q_BLK = 128
K_TILE = 128
ATTN_SCALE = HEAD_DIM ** -0.5
SEL_FORCE = 1e9
N_BUCKETS = 32
REL_MAX_DIST = 1024
N_EXPERTS = 32
TOP_K = 4
SWIGLU_LIMIT = 7.0
SWIGLU_ALPHA = 1.702
DEEPNORM_ALPHA = 2.0 ** 0.25
LN_EPS = 1e-5
IN_PAD = 104
D_IN_PAD = 3968
MOE_ROWS = 256
LANES = 128
NEG = -(2.0 ** 30)
MASKED = -1e8
VMEM_LIMIT = 56 * 1024 * 1024


def _bucket_thresholds():
    max_exact = N_BUCKETS // 2
    ratio = REL_MAX_DIST // max_exact
    assert ratio == 64 and N_BUCKETS - max_exact == 16
    out = list(range(1, max_exact + 1))
    for k in range(1, N_BUCKETS - max_exact):
        n = max_exact
        while n ** 16 < (max_exact ** 16) * (ratio ** k):
            n += 1
        out.append(n)
    return out


BUCKET_THR = _bucket_thresholds()
FAR_DIST = BUCKET_THR[-1]
NEAR_TILES = -(-(FAR_DIST + Q_BLK - 1) // K_TILE)
WIN_TILES = (WINDOW + Q_BLK - 1) // K_TILE + 1
CMP_BAND = 64
assert CMP_STRIDE * (CMP_BAND - 8 - 4) - (CMP_BLOCK - 1) >= FAR_DIST


def _cparams(**kw):
    return pltpu.CompilerParams(vmem_limit_bytes=VMEM_LIMIT, **kw)


def _split_bf16(a):
    hi = a.astype(BF16)
    lo = (a - hi.astype(F32)).astype(BF16)
    return hi, lo


def _dot3(a, b):
    ah, al = _split_bf16(a)
    bh, bl = _split_bf16(b)
    d = functools.partial(jnp.dot, preferred_element_type=F32)
    return d(ah, bh) + (d(ah, bl) + d(al, bh))


def _dot_nt(a, b):
    return lax.dot_general(a, b, (((1,), (1,)), ((), ())), preferred_element_type=F32)


def _ada_kernel(c_ref, w_ref, b_ref, o_ref):
    c = c_ref[...]
    cond = c * jax.nn.sigmoid(c)
    o_ref[...] = _dot3(cond, w_ref[...]) + b_ref[...]


def _ada(c, w_ada, b_ada):
    bsz = c.shape[0]
    c8 = jnp.zeros((8, D_MODEL), F32).at[:bsz].set(c)
    out = pl.pallas_call(
        _ada_kernel,
        grid=(6,),
        in_specs=[pl.BlockSpec((8, D_MODEL), lambda j: (0, 0)),
                  pl.BlockSpec((D_MODEL, D_MODEL), lambda j: (0, j)),
                  pl.BlockSpec((1, D_MODEL), lambda j: (0, j))],
        out_specs=pl.BlockSpec((8, D_MODEL), lambda j: (0, j)),
        out_shape=jax.ShapeDtypeStruct((8, 6 * D_MODEL), F32),
        compiler_params=_cparams(),
        name="ada",
    )(c8, w_ada, b_ada.reshape(1, -1))
    return out[:bsz]


def _inproj_kernel(x_ref, sc_ref, sh_ref, w_ref, u_ref, q_ref, kv_ref, g_ref, gm_ref):
    h = x_ref[...] * (1.0 + sc_ref[0]) + sh_ref[0]
    hb = h.astype(BF16)

    def mm(a, b):
        return jnp.dot(hb, w_ref[:, a:b], preferred_element_type=F32)

    u_ref[...] = mm(0, 512)
    q_ref[...] = (mm(512, 1024) * ATTN_SCALE).astype(BF16)
    kv_ref[...] = mm(1024, 1792).astype(BF16)
    g_ref[...] = jax.nn.sigmoid(mm(1792, 1920))
    gm_ref[:, :1024] = jax.nn.sigmoid(mm(1920, 2944))
    gm_ref[:, 1024:] = jax.nn.sigmoid(mm(2944, 3968))


def _inproj(x2, sc, sh, w_pad, seq):
    t = x2.shape[0]
    tm = 512
    bmap = lambda i: ((i * tm) // seq, 0, 0)
    return pl.pallas_call(
        _inproj_kernel,
        grid=(t // tm,),
        in_specs=[pl.BlockSpec((tm, D_MODEL), lambda i: (i, 0)),
                  pl.BlockSpec((1, 1, D_MODEL), bmap),
                  pl.BlockSpec((1, 1, D_MODEL), bmap),
                  pl.BlockSpec((D_MODEL, D_IN_PAD), lambda i: (0, 0))],
        out_specs=[pl.BlockSpec((tm, 512), lambda i: (i, 0)),
                   pl.BlockSpec((tm, 512), lambda i: (i, 0)),
                   pl.BlockSpec((tm, 768), lambda i: (i, 0)),
                   pl.BlockSpec((tm, 128), lambda i: (i, 0)),
                   pl.BlockSpec((tm, 2048), lambda i: (i, 0))],
        out_shape=[jax.ShapeDtypeStruct((t, 512), F32),
                   jax.ShapeDtypeStruct((t, 512), BF16),
                   jax.ShapeDtypeStruct((t, 768), BF16),
                   jax.ShapeDtypeStruct((t, 128), F32),
                   jax.ShapeDtypeStruct((t, 2048), F32)],
        compiler_params=_cparams(),
        name="inproj",
    )(x2, sc, sh, w_pad)


def _s5_tables(lam_re, lam_im, log_dt, b_re, b_im, c_re, c_im, d_skip):
    hp = lax.Precision.HIGHEST
    L, P, N, G = S5_CHUNK, S5_GROUP, S5_STATE, S5_GROUPS
    lam = lax.complex(lam_re.astype(F32), lam_im.astype(F32))
    dt = jnp.exp(log_dt.astype(F32))[:, None]
    k = jnp.arange(L + 1, dtype=F32)
    pw = jnp.exp((lam * dt)[:, None, :] * k[None, :, None])
    lam_bar = pw[:, 1]
    bb = ((lam_bar - 1.0) / lam)[..., None] * lax.complex(b_re.astype(F32), b_im.astype(F32))
    cm = lax.complex(c_re.astype(F32), c_im.astype(F32))
    klag = jnp.einsum('gpn,gkn,gnq->gkpq', cm, pw[:, :L], bb, precision=hp).real
    s_i = np.arange(L)[:, None]
    t_i = np.arange(L)[None, :]
    lag = t_i - s_i
    m = klag[:, np.clip(lag, 0, L - 1)]
    m = jnp.where(jnp.asarray(lag >= 0)[None, :, :, None, None], m, 0.0)
    m = m.transpose(0, 1, 4, 2, 3).reshape(G, L * P, L * P)
    we = pw[:, L - 1 - np.arange(L)][:, :, :, None] * bb[:, None]
    we = we.transpose(0, 1, 3, 2).reshape(G, L * P, N)
    wi = cm[:, None] * pw[:, 1:L + 1][:, :, None, :]
    wi = wi.transpose(0, 3, 1, 2).reshape(G, N, L * P)
    a = pw[:, L]
    a2 = jnp.stack([a.real, a.imag], axis=1)
    dvec = jnp.tile(d_skip.astype(F32).reshape(G, P), (1, L)).reshape(G, 1, L * P)
    return (m.astype(BF16), we.real.astype(BF16), we.imag.astype(BF16),
            wi.real.astype(BF16), (-wi.imag).astype(BF16), a2, dvec)


def _s5_kernel(bsz, u_ref, m_ref, wer_ref, wei_ref, wir_ref, wii_ref, a_ref, d_ref, y_ref,
               vr_s, vi_s, xr_s, xi_s):
    u = u_ref[0]
    ub = u.astype(BF16)
    d = functools.partial(jnp.dot, preferred_element_type=F32)
    vr_s[...] = d(ub, wer_ref[0])
    vi_s[...] = d(ub, wei_ref[0])
    ar = a_ref[0, 0:1, :]
    ai = a_ref[0, 1:2, :]
    n_chunks = u.shape[0] // bsz

    def body(c, carry):
        xr, xi = carry
        r0 = pl.multiple_of(c * bsz, bsz)
        xr_s[pl.ds(r0, bsz), :] = xr
        xi_s[pl.ds(r0, bsz), :] = xi
        vr = vr_s[pl.ds(r0, bsz), :]
        vi = vi_s[pl.ds(r0, bsz), :]
        return ar * xr - ai * xi + vr, ar * xi + ai * xr + vi

    z = jnp.zeros((bsz, S5_STATE), F32)
    lax.fori_loop(0, n_chunks, body, (z, z))
    y = d(ub, m_ref[0])
    y = y + d(xr_s[...].astype(BF16), wir_ref[0]) + d(xi_s[...].astype(BF16), wii_ref[0])
    y = y + d_ref[0] * u
    y_ref[0] = jax.nn.gelu(y).astype(BF16)


def _s5(u, bsz, seq, tables):
    L, P, G, N = S5_CHUNK, S5_GROUP, S5_GROUPS, S5_STATE
    c = seq // L
    rows = c * bsz
    lp = L * P
    u4 = u.reshape(bsz, c, L, G, P).transpose(3, 1, 0, 2, 4).reshape(G, rows, lp)
    m, wer, wei, wir, wii, a2, dvec = tables
    g3 = lambda g: (g, 0, 0)
    y4 = pl.pallas_call(
        functools.partial(_s5_kernel, bsz),
        grid=(G,),
        in_specs=[pl.BlockSpec((1, rows, lp), g3),
                  pl.BlockSpec((1, lp, lp), g3),
                  pl.BlockSpec((1, lp, N), g3),
                  pl.BlockSpec((1, lp, N), g3),
                  pl.BlockSpec((1, N, lp), g3),
                  pl.BlockSpec((1, N, lp), g3),
                  pl.BlockSpec((1, 2, N), g3),
                  pl.BlockSpec((1, 1, lp), g3)],
        out_specs=pl.BlockSpec((1, rows, lp), g3),
        out_shape=jax.ShapeDtypeStruct((G, rows, lp), BF16),
        scratch_shapes=[pltpu.VMEM((rows, N), F32)] * 4,
        compiler_params=_cparams(),
        name="s5",
    )(u4, m, wer, wei, wir, wii, a2, dvec)
    return y4.reshape(G, c, bsz, L, P).transpose(2, 1, 3, 0, 4).reshape(bsz * seq, S5_WIDTH)


def _cmp_kernel(k_ref, pos_ref, w1_ref, w2_ref, o_ref):
    k = k_ref[0, 0, 0].astype(F32)
    nc = k.shape[0]
    half = CMP_STRIDE * HEAD_DIM
    d = functools.partial(jnp.dot, preferred_element_type=F32)
    a = d((k + pos_ref[0, 0:1, :]).astype(BF16), w1_ref[0, :half, :])
    b = d((k + pos_ref[0, 1:2, :]).astype(BF16), w1_ref[0, half:, :])
    pre = a + pltpu.roll(b, shift=nc - 1, axis=0)
    o_ref[0, 0, 0] = d(jax.nn.gelu(pre).astype(BF16), w2_ref[0]).astype(BF16)


def _compress(cin, pos2, w1, w2):
    _, bsz, g, nc, wid = cin.shape
    return pl.pallas_call(
        _cmp_kernel,
        grid=(2, bsz, g),
        in_specs=[pl.BlockSpec((1, 1, 1, nc, wid), lambda a, b, c: (a, b, c, 0, 0)),
                  pl.BlockSpec((1, 2, wid), lambda a, b, c: (a, 0, 0)),
                  pl.BlockSpec((1, 2 * wid, CMP_HIDDEN), lambda a, b, c: (a, 0, 0)),
                  pl.BlockSpec((1, CMP_HIDDEN, HEAD_DIM), lambda a, b, c: (a, 0, 0))],
        out_specs=pl.BlockSpec((1, 1, 1, nc, HEAD_DIM), lambda a, b, c: (a, b, c, 0, 0)),
        out_shape=jax.ShapeDtypeStruct((2, bsz, g, nc, HEAD_DIM), BF16),
        compiler_params=_cparams(),
        name="compress",
    )(cin, pos2, w1, w2)


def _bias_of(dist, tbl_ref, head):
    n = jnp.maximum(dist, 0)
    val = jnp.full(dist.shape, tbl_ref[0, head], F32)
    for b, thr in enumerate(BUCKET_THR, start=1):
        val = jnp.where(n >= thr, tbl_ref[b, head], val)
    return val


def _tables_kernel(tbl_ref, bt_ref, wt_ref, t2_ref, qc_ref):
    g = pl.program_id(0)
    r = lax.broadcasted_iota(jnp.int32, (Q_BLK, K_TILE), 0)
    c = lax.broadcasted_iota(jnp.int32, (Q_BLK, K_TILE), 1)
    for h in range(HPG):
        head = g * HPG + h
        rows = slice(h * Q_BLK, (h + 1) * Q_BLK)
        c_far = tbl_ref[N_BUCKETS - 1, head]
        for dl in range(NEAR_TILES):
            dist = dl * K_TILE + r - c
            bt_ref[0, dl, rows, :] = jnp.where(dist >= 0, _bias_of(dist, tbl_ref, head) - c_far, NEG)
        for dl in range(WIN_TILES):
            dist = dl * K_TILE + r - c
            ok = (dist >= 0) & (dist < WINDOW)
            wt_ref[0, dl, rows, :] = jnp.where(ok, _bias_of(dist, tbl_ref, head), NEG)
        dist = CMP_STRIDE * (c - 7) + r - (CMP_BLOCK - 1)
        band = jnp.where(dist >= 0, _bias_of(dist, tbl_ref, head) - c_far, NEG)
        band = jnp.where(c < CMP_BAND, band, 0.0)
        hi = band.astype(BF16).astype(F32)
        lo = (band - hi).astype(BF16).astype(F32)
        lo = pltpu.roll(lo, shift=CMP_BAND, axis=1)
        t2_ref[0, rows, 0:K_TILE] = jnp.where(c < CMP_BAND, hi, lo).astype(BF16)
        t2_ref[0, rows, K_TILE:2 * K_TILE] = jnp.where(c == 0, NEG, 0.0).astype(BF16)
        cf = jnp.full((Q_BLK, K_TILE), c_far, F32)
        cf_hi = cf.astype(BF16).astype(F32)
        cf_lo = (cf - cf_hi).astype(BF16).astype(F32)
        qc_ref[0, rows, :] = jnp.where(c == 0, cf_hi, jnp.where(c == 1, cf_lo, 0.0)).astype(BF16)


def _bias_tables(rel_bias):
    g = N_KV_GROUPS
    rows = HPG * Q_BLK
    return pl.pallas_call(
        _tables_kernel,
        grid=(g,),
        in_specs=[pl.BlockSpec(memory_space=pltpu.SMEM)],
        out_specs=[pl.BlockSpec((1, NEAR_TILES, rows, K_TILE), lambda i: (i, 0, 0, 0)),
                   pl.BlockSpec((1, WIN_TILES, rows, K_TILE), lambda i: (i, 0, 0, 0)),
                   pl.BlockSpec((1, rows, 2 * K_TILE), lambda i: (i, 0, 0)),
                   pl.BlockSpec((1, rows, K_TILE), lambda i: (i, 0, 0))],
        out_shape=[jax.ShapeDtypeStruct((g, NEAR_TILES, rows, K_TILE), F32),
                   jax.ShapeDtypeStruct((g, WIN_TILES, rows, K_TILE), F32),
                   jax.ShapeDtypeStruct((g, rows, 2 * K_TILE), BF16),
                   jax.ShapeDtypeStruct((g, rows, K_TILE), BF16)],
        compiler_params=_cparams(),
        name="bias_tables",
    )(rel_bias.astype(F32))


def _softmax_step(s, v, state):
    m_raw, l, acc = state
    m_new = jnp.maximum(m_raw, jnp.max(s, axis=1, keepdims=True))
    m_use = jnp.where(m_new < MASKED, 0.0, m_new)
    alpha = jnp.exp(m_raw - m_use)
    p = jnp.exp(s - m_use)
    l = alpha * l + jnp.sum(p, axis=1, keepdims=True)
    acc = alpha * acc + jnp.dot(p.astype(BF16), v, preferred_element_type=F32)
    return m_new, l, acc


def _attn_kernel(q_ref, g_ref, kc_ref, vc_ref, ks_ref, vs_ref, kw_ref, vw_ref,
                 bt_ref, wt_ref, t2_ref, ov_ref, o_ref, qa_s):
    i = pl.program_id(2)
    rows = HPG * Q_BLK
    qx = q_ref[0, 0, 0]
    nc = kc_ref.shape[2]
    dot = functools.partial(jnp.dot, preferred_element_type=F32)

    sc = _dot_nt(qx, kc_ref[0, 0])
    rr = lax.broadcasted_iota(jnp.int32, (2 * K_TILE, nc), 0)
    cc = lax.broadcasted_iota(jnp.int32, (2 * K_TILE, nc), 1)
    tgt = 8 * i + 7
    place = ((rr < K_TILE) & (cc == tgt - (rr & (CMP_BAND - 1)))) | ((rr == K_TILE) & (cc >= tgt))
    sc = sc + dot(t2_ref[0], jnp.where(place, 1.0, 0.0).astype(BF16))
    mc = jnp.max(sc, axis=1, keepdims=True)
    mc = jnp.where(mc < MASKED, 0.0, mc)
    pc = jnp.exp(sc - mc)
    pc = pc / jnp.maximum(jnp.sum(pc, axis=1, keepdims=True), 1e-30)
    o_c = dot(pc.astype(BF16), vc_ref[0, 0])

    ps = pc[0:Q_BLK] + pc[Q_BLK:2 * Q_BLK] + pc[2 * Q_BLK:3 * Q_BLK] + pc[3 * Q_BLK:4 * Q_BLK]
    ps_hi, ps_lo = _split_bf16(ps)
    imp = dot(ps_hi, ov_ref[...]) + dot(ps_lo, ov_ref[...])
    imp_t = imp.T
    jb = lax.broadcasted_iota(jnp.int32, (LANES, Q_BLK), 0)
    tq = lax.broadcasted_iota(jnp.int32, (LANES, Q_BLK), 1)
    cur = 2 * i + jnp.where(tq >= SEL_BLOCK, 1, 0)
    forced = (jb == 0) | (jb == cur) | (jb == cur - 1)
    score = jnp.where(forced, SEL_FORCE, jnp.where(jb <= cur, imp_t, -SEL_FORCE))
    picked = jnp.zeros((LANES, Q_BLK), jnp.bool_)
    for _ in range(SEL_TOPK):
        mx = jnp.max(score, axis=0, keepdims=True)
        first = jnp.min(jnp.where(score == mx, jb, LANES), axis=0, keepdims=True)
        hit = jb == first
        picked = picked | hit
        score = jnp.where(hit, -jnp.inf, score)
    add_mask = jnp.where(picked, 0.0, NEG).T.astype(BF16)
    qa_s[:, 0:LANES] = jnp.concatenate([add_mask] * HPG, axis=0)
    qa_s[:, LANES:2 * LANES] = qx
    qa = qa_s[...]

    init = (jnp.full((rows, 1), NEG, F32), jnp.zeros((rows, 1), F32), jnp.zeros((rows, HEAD_DIM), F32))

    def far_body(kt, state):
        k0 = pl.multiple_of(kt * K_TILE, K_TILE)
        s = _dot_nt(qa, ks_ref[0, 0, pl.ds(k0, K_TILE), :])
        return _softmax_step(s, vs_ref[0, 0, pl.ds(k0, K_TILE), :], state)

    def near_body(dl, state):
        k0 = pl.multiple_of((i - dl) * K_TILE, K_TILE)
        s = _dot_nt(qa, ks_ref[0, 0, pl.ds(k0, K_TILE), :]) + bt_ref[0, dl]
        return _softmax_step(s, vs_ref[0, 0, pl.ds(k0, K_TILE), :], state)

    st = lax.fori_loop(0, jnp.maximum(i - (NEAR_TILES - 1), 0), far_body, init)
    _, l_s, acc_s = lax.fori_loop(0, jnp.minimum(i, NEAR_TILES - 1) + 1, near_body, st)

    def win_body(dl, state):
        k0 = pl.multiple_of((i - dl) * K_TILE, K_TILE)
        s = _dot_nt(qx, kw_ref[0, 0, pl.ds(k0, K_TILE), :]) + wt_ref[0, dl]
        return _softmax_step(s, vw_ref[0, 0, pl.ds(k0, K_TILE), :], state)

    _, l_w, acc_w = lax.fori_loop(0, jnp.minimum(i, WIN_TILES - 1) + 1, win_body, init)

    gb = g_ref[0, 0, 0]
    o_s = acc_s / jnp.maximum(l_s, 1e-30)
    o_w = acc_w / jnp.maximum(l_w, 1e-30)
    o_ref[0, 0, 0] = (gb[:, 0:1] * o_c + gb[:, 1:2] * o_s + gb[:, 2:3] * o_w).astype(BF16)


def _attention(q_ext, g_hq, kc_ext, vc, ks_aug, vs, kw_ext, vw, bt, wt, t2, ov):
    bsz, g, nq, rows, _ = q_ext.shape
    seq = ks_aug.shape[2]
    nc = kc_ext.shape[2]
    per_bg = lambda b, c, i: (b, c, 0, 0)
    per_g4 = lambda b, c, i: (c, 0, 0, 0)
    return pl.pallas_call(
        _attn_kernel,
        grid=(bsz, g, nq),
        in_specs=[pl.BlockSpec((1, 1, 1, rows, LANES), lambda b, c, i: (b, c, i, 0, 0)),
                  pl.BlockSpec((1, 1, 1, rows, 3), lambda b, c, i: (b, c, i, 0, 0)),
                  pl.BlockSpec((1, 1, nc, LANES), per_bg),
                  pl.BlockSpec((1, 1, nc, HEAD_DIM), per_bg),
                  pl.BlockSpec((1, 1, seq, 2 * LANES), per_bg),
                  pl.BlockSpec((1, 1, seq, HEAD_DIM), per_bg),
                  pl.BlockSpec((1, 1, seq, LANES), per_bg),
                  pl.BlockSpec((1, 1, seq, HEAD_DIM), per_bg),
                  pl.BlockSpec((1, NEAR_TILES, rows, K_TILE), per_g4),
                  pl.BlockSpec((1, WIN_TILES, rows, K_TILE), per_g4),
                  pl.BlockSpec((1, rows, 2 * K_TILE), lambda b, c, i: (c, 0, 0)),
                  pl.BlockSpec((nc, LANES), lambda b, c, i: (0, 0))],
        out_specs=pl.BlockSpec((1, 1, 1, rows, HEAD_DIM), lambda b, c, i: (b, c, i, 0, 0)),
        out_shape=jax.ShapeDtypeStruct((bsz, g, nq, rows, HEAD_DIM), BF16),
        scratch_shapes=[pltpu.VMEM((rows, 2 * LANES), BF16)],
        compiler_params=_cparams(),
        name="nsa_attention",
    )(q_ext, g_hq, kc_ext, vc, ks_aug, vs, kw_ext, vw, bt, wt, t2, ov)


def _overlap_matrix(nc):
    n = np.arange(nc)[:, None]
    j = np.arange(LANES)[None, :]
    cs, ce = n * CMP_STRIDE, n * CMP_STRIDE + CMP_BLOCK - 1
    ss, se = j * SEL_BLOCK, j * SEL_BLOCK + SEL_BLOCK - 1
    ov = (cs <= se) & (ce >= ss) & (n < nc - 1)
    return jnp.asarray(ov.astype(np.float32), BF16)


def _layer_norm(z, g, b):
    mu = jnp.mean(z, axis=-1, keepdims=True)
    zc = z - mu
    var = jnp.mean(zc * zc, axis=-1, keepdims=True)
    return zc * lax.rsqrt(var + LN_EPS) * g + b


def _mix_kernel(ys_ref, on_ref, gm_ref, x_ref, ga_ref, scf_ref, shf_ref, wglu_ref, bglu_ref, wup_ref,
                wout_ref, lg_ref, lb_ref, wr_ref, br_ref, x1_ref, h2_ref, e_ref):
    d = functools.partial(jnp.dot, preferred_element_type=F32)
    glu = d(ys_ref[...], wglu_ref[...]) + bglu_ref[...]
    y_s5 = glu[:, :D_MODEL] * jax.nn.sigmoid(glu[:, D_MODEL:])
    y_nsa = d(on_ref[...], wup_ref[...])
    merged = gm_ref[:, :D_MODEL] * y_s5 + gm_ref[:, D_MODEL:] * y_nsa
    y = d(merged.astype(BF16), wout_ref[...])
    x1 = _layer_norm(DEEPNORM_ALPHA * x_ref[...] + ga_ref[0] * y, lg_ref[...], lb_ref[...])
    x1_ref[...] = x1
    h2 = x1 * (1.0 + scf_ref[0]) + shf_ref[0]
    h2_ref[...] = h2
    logits = _dot3(h2, wr_ref[...]) + br_ref[...]
    col = lax.broadcasted_iota(jnp.int32, logits.shape, 1)
    lg = jnp.where(col < N_EXPERTS, logits, -jnp.inf)
    vals, hits = [], []
    for _ in range(TOP_K):
        mx = jnp.max(lg, axis=1, keepdims=True)
        first = jnp.min(jnp.where(lg == mx, col, LANES), axis=1, keepdims=True)
        hit = col == first
        vals.append(mx)
        hits.append(hit)
        lg = jnp.where(hit, -jnp.inf, lg)
    ex = [jnp.exp(v - vals[0]) for v in vals]
    den = ex[0] + ex[1] + ex[2] + ex[3]
    e = jnp.zeros(logits.shape, F32)
    for hit, w in zip(hits, ex):
        e = jnp.where(hit, w / den, e)
    e_ref[...] = e


def _mix(ys5, onsa, gm, x2, ga, scf, shf, wglu, bglu, wup, wout, ln_g, ln_b, wr, br, seq):
    t = x2.shape[0]
    tm = 256
    row = lambda i: (i, 0)
    const = lambda i: (0, 0)
    bmap = lambda i: ((i * tm) // seq, 0, 0)
    return pl.pallas_call(
        _mix_kernel,
        grid=(t // tm,),
        in_specs=[pl.BlockSpec((tm, 512), row), pl.BlockSpec((tm, 512), row),
                  pl.BlockSpec((tm, 2048), row), pl.BlockSpec((tm, D_MODEL), row),
                  pl.BlockSpec((1, 1, D_MODEL), bmap), pl.BlockSpec((1, 1, D_MODEL), bmap),
                  pl.BlockSpec((1, 1, D_MODEL), bmap),
                  pl.BlockSpec((512, 2048), const), pl.BlockSpec((1, 2048), const),
                  pl.BlockSpec((512, D_MODEL), const), pl.BlockSpec((D_MODEL, D_MODEL), const),
                  pl.BlockSpec((1, D_MODEL), const), pl.BlockSpec((1, D_MODEL), const),
                  pl.BlockSpec((D_MODEL, LANES), const), pl.BlockSpec((1, LANES), const)],
        out_specs=[pl.BlockSpec((tm, D_MODEL), row), pl.BlockSpec((tm, D_MODEL), row),
                   pl.BlockSpec((tm, LANES), row)],
        out_shape=[jax.ShapeDtypeStruct((t, D_MODEL), F32), jax.ShapeDtypeStruct((t, D_MODEL), F32),
                   jax.ShapeDtypeStruct((t, LANES), F32)],
        compiler_params=_cparams(),
        name="mix_ln1_router",
    )(ys5, onsa, gm, x2, ga, scf, shf, wglu, bglu, wup, wout, ln_g, ln_b, wr, br)


def _pos_kernel(e_ref, p_ref, cnt_ref, carry_s):
    @pl.when(pl.program_id(0) == 0)
    def _():
        carry_s[...] = jnp.zeros_like(carry_s)

    tm = e_ref.shape[0]
    onehot = jnp.where(e_ref[...] > 0.0, 1.0, 0.0)
    r = lax.broadcasted_iota(jnp.int32, (tm, tm), 0)
    c = lax.broadcasted_iota(jnp.int32, (tm, tm), 1)
    tri = jnp.where(c < r, 1.0, 0.0).astype(BF16)
    p_ref[...] = carry_s[...] + jnp.dot(tri, onehot.astype(BF16), preferred_element_type=F32)
    carry_s[...] = carry_s[...] + jnp.sum(onehot, axis=0, keepdims=True)
    cnt_ref[...] = carry_s[...]


def _positions(e):
    t = e.shape[0]
    tm = 512
    return pl.pallas_call(
        _pos_kernel,
        grid=(t // tm,),
        in_specs=[pl.BlockSpec((tm, LANES), lambda i: (i, 0))],
        out_specs=[pl.BlockSpec((tm, LANES), lambda i: (i, 0)), pl.BlockSpec((1, LANES), lambda i: (0, 0))],
        out_shape=[jax.ShapeDtypeStruct((t, LANES), F32), jax.ShapeDtypeStruct((1, LANES), F32)],
        scratch_shapes=[pltpu.VMEM((1, LANES), F32)],
        compiler_params=_cparams(),
        name="moe_positions",
    )(e)


def _dest_kernel(e_ref, p_ref, st_ref, d_ref, g_ref):
    e = e_ref[...]
    big = 1e9
    dm = jnp.where(e > 0.0, st_ref[...] + p_ref[...], big)
    col = lax.broadcasted_iota(jnp.int32, e.shape, 1)
    dest = jnp.zeros(e.shape, F32)
    gate = jnp.zeros(e.shape, F32)
    for k in range(TOP_K):
        dk = jnp.min(dm, axis=1, keepdims=True)
        hit = dm == dk
        gk = jnp.sum(jnp.where(hit, e, 0.0), axis=1, keepdims=True)
        dest = jnp.where(col == k, dk, dest)
        gate = jnp.where(col == k, gk, gate)
        dm = jnp.where(hit, big, dm)
    d_ref[...] = dest.astype(jnp.int32)
    g_ref[...] = gate


def _destinations(e, p, starts):
    t = e.shape[0]
    tm = 512
    row = lambda i: (i, 0)
    return pl.pallas_call(
        _dest_kernel,
        grid=(t // tm,),
        in_specs=[pl.BlockSpec((tm, LANES), row), pl.BlockSpec((tm, LANES), row),
                  pl.BlockSpec((1, LANES), lambda i: (0, 0))],
        out_specs=[pl.BlockSpec((tm, LANES), row), pl.BlockSpec((tm, LANES), row)],
        out_shape=[jax.ShapeDtypeStruct((t, LANES), jnp.int32), jax.ShapeDtypeStruct((t, LANES), F32)],
        compiler_params=_cparams(),
        name="moe_destinations",
    )(e, p, starts)


def _row_copy(src_ref, src_row, dst_ref, dst_row, sem):
    return pltpu.make_async_copy(src_ref.at[pl.ds(src_row, 1)], dst_ref.at[pl.ds(dst_row, 1)], sem)


def _dispatch_kernel(dest_ref, h_ref, xs_in_ref, xs_ref, sem):
    del xs_in_ref
    tm = h_ref.shape[0]

    def start(r, _):
        for k in range(TOP_K):
            _row_copy(h_ref, r, xs_ref, dest_ref[r * TOP_K + k], sem).start()
        return 0

    def wait(r, _):
        for k in range(TOP_K):
            _row_copy(h_ref, 0, xs_ref, 0, sem).wait()
        return 0

    lax.fori_loop(0, tm, start, 0)
    lax.fori_loop(0, tm, wait, 0)


def _dispatch(dest_flat, h2, n_rows):
    t = h2.shape[0]
    tm = 256
    xs0 = jnp.zeros((n_rows, D_MODEL), F32)
    return pl.pallas_call(
        _dispatch_kernel,
        grid=(t // tm,),
        in_specs=[pl.BlockSpec((tm * TOP_K,), lambda i: (i,), memory_space=pltpu.SMEM),
                  pl.BlockSpec((tm, D_MODEL), lambda i: (i, 0)),
                  pl.BlockSpec(memory_space=pl.ANY)],
        out_specs=pl.BlockSpec(memory_space=pl.ANY),
        out_shape=jax.ShapeDtypeStruct((n_rows, D_MODEL), F32),
        scratch_shapes=[pltpu.SemaphoreType.DMA(())],
        input_output_aliases={2: 0},
        compiler_params=_cparams(),
        name="moe_dispatch",
    )(dest_flat, h2, xs0)


def _expert_kernel(be_ref, bv_ref, x_ref, wgu_ref, bgu_ref, wd_ref, bd_ref, y_ref):
    del be_ref
    i = pl.program_id(0)

    @pl.when(bv_ref[i] > 0)
    def _():
        d = functools.partial(jnp.dot, preferred_element_type=F32)
        gu = d(x_ref[...].astype(BF16), wgu_ref[0]) + bgu_ref[0]
        gate = jnp.minimum(gu[:, :D_MODEL], SWIGLU_LIMIT)
        up = jnp.clip(gu[:, D_MODEL:], -SWIGLU_LIMIT, SWIGLU_LIMIT)
        act = (up + 1.0) * gate * jax.nn.sigmoid(SWIGLU_ALPHA * gate)
        y_ref[...] = d(act.astype(BF16), wd_ref[0]) + bd_ref[0]

    @pl.when(bv_ref[i] == 0)
    def _():
        y_ref[...] = jnp.zeros_like(y_ref)


def _experts(blk_e, blk_valid, xs, wgu, bgu, wd, bd):
    n_rows = xs.shape[0]
    n_blk = n_rows // MOE_ROWS
    grid_spec = pltpu.PrefetchScalarGridSpec(
        num_scalar_prefetch=2,
        grid=(n_blk,),
        in_specs=[pl.BlockSpec((MOE_ROWS, D_MODEL), lambda i, be, bv: (i, 0)),
                  pl.BlockSpec((1, D_MODEL, 2 * D_MODEL), lambda i, be, bv: (be[i], 0, 0)),
                  pl.BlockSpec((1, 1, 2 * D_MODEL), lambda i, be, bv: (be[i], 0, 0)),
                  pl.BlockSpec((1, D_MODEL, D_MODEL), lambda i, be, bv: (be[i], 0, 0)),
                  pl.BlockSpec((1, 1, D_MODEL), lambda i, be, bv: (be[i], 0, 0))],
        out_specs=pl.BlockSpec((MOE_ROWS, D_MODEL), lambda i, be, bv: (i, 0)),
    )
    return pl.pallas_call(
        _expert_kernel,
        grid_spec=grid_spec,
        out_shape=jax.ShapeDtypeStruct((n_rows, D_MODEL), F32),
        compiler_params=_cparams(),
        name="moe_experts",
    )(blk_e, blk_valid, xs, wgu, bgu, wd, bd)


def _final_kernel(dest_ref, gate_ref, x1_ref, gf_ref, lg_ref, lb_ref, y_ref, o_ref, buf, sem):
    tm = x1_ref.shape[0]

    def start(r, _):
        for k in range(TOP_K):
            pltpu.make_async_copy(y_ref.at[pl.ds(dest_ref[r * TOP_K + k], 1)],
                                  buf.at[k, pl.ds(r, 1)], sem).start()
        return 0

    def wait(r, _):
        for k in range(TOP_K):
            pltpu.make_async_copy(y_ref.at[pl.ds(0, 1)], buf.at[0, pl.ds(0, 1)], sem).wait()
        return 0

    lax.fori_loop(0, tm, start, 0)
    lax.fori_loop(0, tm, wait, 0)
    gate = gate_ref[...]
    y = gate[:, 0:1] * buf[0]
    for k in range(1, TOP_K):
        y = y + gate[:, k:k + 1] * buf[k]
    o_ref[...] = _layer_norm(DEEPNORM_ALPHA * x1_ref[...] + gf_ref[0] * y, lg_ref[...], lb_ref[...])


def _combine(dest_flat, gate, x1, gf, ln_g, ln_b, y_rows, seq):
    t = x1.shape[0]
    tm = 256
    bmap = lambda i: ((i * tm) // seq, 0, 0)
    const = lambda i: (0, 0)
    return pl.pallas_call(
        _final_kernel,
        grid=(t // tm,),
        in_specs=[pl.BlockSpec((tm * TOP_K,), lambda i: (i,), memory_space=pltpu.SMEM),
                  pl.BlockSpec((tm, LANES), lambda i: (i, 0)),
                  pl.BlockSpec((tm, D_MODEL), lambda i: (i, 0)),
                  pl.BlockSpec((1, 1, D_MODEL), bmap),
                  pl.BlockSpec((1, D_MODEL), const), pl.BlockSpec((1, D_MODEL), const),
                  pl.BlockSpec(memory_space=pl.ANY)],
        out_specs=pl.BlockSpec((tm, D_MODEL), lambda i: (i, 0)),
        out_shape=jax.ShapeDtypeStruct((t, D_MODEL), F32),
        scratch_shapes=[pltpu.VMEM((TOP_K, tm, D_MODEL), F32), pltpu.SemaphoreType.DMA(())],
        compiler_params=_cparams(),
        name="moe_combine_ln2",
    )(dest_flat, gate, x1, gf, ln_g, ln_b, y_rows)


def _layer(x, cond_ada, rel_bias, w_in, s5p, w_glu, b_glu, cmp_k, cmp_v, w_nsa_up, w_out, ln1, w_router,
           b_router, w_gate_up, b_gate_up, w_down, b_down, ln2):
    bsz, seq, _ = x.shape
    t = bsz * seq
    g, nq, nc = N_KV_GROUPS, seq // Q_BLK, seq // CMP_STRIDE
    assert seq % 512 == 0 and seq // SEL_BLOCK <= LANES and nc - 1 == 8 * (nq - 1) + 7
    sh_a, sc_a, g_a, sh_f, sc_f, g_f = [v.reshape(bsz, 1, D_MODEL) for v in jnp.split(cond_ada, 6, axis=-1)]
    x2 = x.reshape(t, D_MODEL)

    split = 1024 + 768 + 3 * N_Q_HEADS
    w_pad = jnp.concatenate([w_in[:, :split], jnp.zeros((D_MODEL, IN_PAD), w_in.dtype), w_in[:, split:]],
                            axis=1).astype(BF16)
    u, q, kv, gbr, gm = _inproj(x2, sc_a, sh_a, w_pad, seq)

    ys5 = _s5(u, bsz, seq, _s5_tables(*s5p))

    kvt = kv.reshape(bsz, seq, 6, g, HEAD_DIM).transpose(2, 0, 3, 1, 4)
    cin = kvt[0:2].reshape(2, bsz, g, nc, CMP_STRIDE * HEAD_DIM)
    pos2 = jnp.stack([cmp_k[0], cmp_v[0]]).reshape(2, 2, CMP_STRIDE * HEAD_DIM).astype(F32)
    w1 = jnp.stack([cmp_k[1], cmp_v[1]]).astype(BF16)
    w2 = jnp.stack([cmp_k[2], cmp_v[2]]).astype(BF16)
    kcvc = _compress(cin, pos2, w1, w2)

    bt, wt, t2, qc = _bias_tables(rel_bias)

    def ones_pad(k):
        shp = k.shape[:-1]
        return jnp.concatenate([k, jnp.ones(shp + (2,), BF16), jnp.zeros(shp + (LANES - HEAD_DIM - 2,), BF16)], -1)

    ind = (np.arange(seq)[:, None] // SEL_BLOCK == np.arange(LANES)[None, :]).astype(np.float32)
    ks_aug = jnp.concatenate([jnp.broadcast_to(jnp.asarray(ind, BF16), (bsz, g, seq, LANES)), ones_pad(kvt[2])], -1)
    kw_ext = jnp.concatenate([kvt[4], jnp.zeros((bsz, g, seq, LANES - HEAD_DIM), BF16)], -1)
    kc_ext = ones_pad(kcvc[0])
    q_hq = (q.reshape(bsz, nq, Q_BLK, g, HPG, HEAD_DIM).transpose(0, 3, 1, 4, 2, 5)
            .reshape(bsz, g, nq, HPG * Q_BLK, HEAD_DIM))
    q_ext = jnp.concatenate([q_hq, jnp.broadcast_to(qc[None, :, None, :, :HEAD_DIM],
                                                    (bsz, g, nq, HPG * Q_BLK, HEAD_DIM))], -1)
    g_hq = (gbr[:, :3 * N_Q_HEADS].reshape(bsz, nq, Q_BLK, g, HPG, 3).transpose(0, 3, 1, 4, 2, 5)
            .reshape(bsz, g, nq, HPG * Q_BLK, 3))
    o_hq = _attention(q_ext, g_hq, kc_ext, kcvc[1], ks_aug, kvt[3], kw_ext, kvt[5], bt, wt, t2,
                      _overlap_matrix(nc))
    onsa = (o_hq.reshape(bsz, g, nq, HPG, Q_BLK, HEAD_DIM).transpose(0, 2, 4, 1, 3, 5)
            .reshape(t, NSA_WIDTH))

    wr = jnp.zeros((D_MODEL, LANES), F32).at[:, :N_EXPERTS].set(w_router.astype(F32))
    br = jnp.zeros((1, LANES), F32).at[0, :N_EXPERTS].set(b_router.astype(F32))
    x1, h2, e = _mix(ys5, onsa, gm, x2, g_a, sc_f, sh_f, w_glu.astype(BF16), b_glu.reshape(1, -1).astype(F32),
                     w_nsa_up.astype(BF16), w_out.astype(BF16), ln1[0].reshape(1, -1), ln1[1].reshape(1, -1),
                     wr, br, seq)

    pos, counts = _positions(e)
    pcounts = jnp.ceil(counts / MOE_ROWS) * MOE_ROWS
    pends = jnp.cumsum(pcounts, axis=1)
    starts = pends - pcounts
    dest, gate = _destinations(e, pos, starts)
    dest_flat = dest[:, :TOP_K].reshape(-1)
    n_blk = t * TOP_K // MOE_ROWS + N_EXPERTS
    blk_row = jnp.arange(n_blk, dtype=F32) * MOE_ROWS
    pe = pends[0, :N_EXPERTS]
    blk_e = jnp.minimum(jnp.sum(blk_row[:, None] >= pe[None, :], axis=1), N_EXPERTS - 1).astype(jnp.int32)
    blk_valid = (blk_row < pe[-1]).astype(jnp.int32)
    xs = _dispatch(dest_flat, h2, n_blk * MOE_ROWS)
    y_rows = _experts(blk_e, blk_valid, xs, w_gate_up.astype(BF16), b_gate_up.reshape(N_EXPERTS, 1, -1).astype(F32),
                      w_down.astype(BF16), b_down.reshape(N_EXPERTS, 1, -1).astype(F32))
    out = _combine(dest_flat, gate, x1, g_f, ln2[0].reshape(1, -1), ln2[1].reshape(1, -1), y_rows, seq)
    return out.reshape(bsz, seq, D_MODEL)


def kernel(x, c, rel_bias, w_ada, b_ada, w_in, ssm_lam_re, ssm_lam_im, ssm_log_dt, ssm_b_re, ssm_b_im, ssm_c_re, ssm_c_im, ssm_d, w_glu, b_glu, cmp_pos_k, cmp_w1_k, cmp_w2_k, cmp_pos_v, cmp_w1_v, cmp_w2_v, w_nsa_up, w_out, ln1_g, ln1_b, w_router, b_router, w_gate_up, b_gate_up, w_down, b_down, ln2_g, ln2_b):
    depth = w_ada.shape[0]
    for l in range(depth):
        ada = _ada(c, w_ada[l], b_ada[l])
        x = _layer(x, ada, rel_bias, w_in[l],
                   (ssm_lam_re[l], ssm_lam_im[l], ssm_log_dt[l], ssm_b_re[l], ssm_b_im[l], ssm_c_re[l],
                    ssm_c_im[l], ssm_d[l]),
                   w_glu[l], b_glu[l], (cmp_pos_k[l], cmp_w1_k[l], cmp_w2_k[l]),
                   (cmp_pos_v[l], cmp_w1_v[l], cmp_w2_v[l]), w_nsa_up[l], w_out[l], (ln1_g[l], ln1_b[l]),
                   w_router[l], b_router[l], w_gate_up[l], b_gate_up[l], w_down[l], b_down[l],
                   (ln2_g[l], ln2_b[l]))
    return x
```
